```python
import math
import jax, jax.numpy as jnp
from jax import lax
import numpy as np

D_MODEL = 1024
BATCH = 8
SEQ = 2048
DEPTH = 1
DEC_BATCH = 128
DEC_SEQ = 4
PAST_LEN = 2048
PAGE_SIZE = 128

N_HEADS = 8
N_KV_HEADS = 2
HEAD_DIM = 64
GROUP = N_HEADS // N_KV_HEADS
ATTN_WIDTH = N_HEADS * HEAD_DIM
KV_WIDTH = 2 * N_KV_HEADS * HEAD_DIM
CMP_BLOCK = 32
CMP_STRIDE = 16
CMP_HIDDEN = 128
SEL_BLOCK = 64
TOP_N = 8
WINDOW = 512
Q_BLOCK = 128
NUM_BUCKETS = 32
MAX_EXACT = 16
MAX_DISTANCE = 128
SSM_WIDTH = 512
SSM_GROUP_CH = 16
N_SSM_GROUPS = SSM_WIDTH // SSM_GROUP_CH
SSM_STATE = 64
DT_MIN = 0.001
DT_MAX = 0.1
D_FF = 2816
PLE_DIM = 256
RMS_EPS = 1e-6
NEG_INF = -1e30
BIG = 1e9
SPLIT_POINTS = (ATTN_WIDTH, ATTN_WIDTH + KV_WIDTH, ATTN_WIDTH + 2 * KV_WIDTH, ATTN_WIDTH + 3 * KV_WIDTH,
                ATTN_WIDTH + 3 * KV_WIDTH + 3 * N_HEADS, ATTN_WIDTH + 3 * KV_WIDTH + 3 * N_HEADS + SSM_WIDTH)
IN_WIDTH = SPLIT_POINTS[-1] + 2 * D_MODEL

kernel_name = 'nsa_s5_macaron_hybrid_step'


def rmsnorm(x, g):
    xf = x.astype(jnp.float32)
    var = jnp.mean(xf * xf, axis=-1, keepdims=True)
    return (xf * lax.rsqrt(var + RMS_EPS) * g.astype(jnp.float32)).astype(x.dtype)


def half_ffn(x, g, w_gate, w_up, w_down):
    h = rmsnorm(x, g)
    return x + 0.5 * ((jax.nn.silu(h @ w_gate) * (h @ w_up)) @ w_down)


def ple_add(x, p, g, w_gate, w_proj):
    return x + jax.nn.sigmoid(rmsnorm(x, g) @ w_gate) * (p @ w_proj)


def t5_bucket(dist):
    n = jnp.maximum(dist, 0)
    nf = jnp.maximum(n, 1).astype(jnp.float32)
    large = MAX_EXACT + (jnp.log(nf / MAX_EXACT) / math.log(MAX_DISTANCE / MAX_EXACT)
                         * (NUM_BUCKETS - MAX_EXACT)).astype(jnp.int32)
    large = jnp.minimum(large, NUM_BUCKETS - 1)
    return jnp.where(n < MAX_EXACT, n, large)


def masked_softmax(logits, mask):
    logits = jnp.where(mask, logits.astype(jnp.float32), NEG_INF)
    return jax.nn.softmax(logits, axis=-1) * mask


def compress_rows(rows, pos, w1, b1, w2):
    bn, length = rows.shape[:2]
    n_cmp = (length - CMP_BLOCK) // CMP_STRIDE + 1
    idx = jnp.arange(n_cmp)[:, None] * CMP_STRIDE + jnp.arange(CMP_BLOCK)[None, :]
    blk = rows[:, idx] + pos[None, None, :, None, :]
    blk = jnp.swapaxes(blk, 2, 3).reshape(bn, n_cmp, N_KV_HEADS, CMP_BLOCK * HEAD_DIM)
    return jax.nn.gelu(blk @ w1 + b1) @ w2


def nsa_context(kv_cmp_rows, kv_sel_rows, cmp_params):
    pos_k, w1_k, b1_k, w2_k, pos_v, w1_v, b1_v, w2_v = cmp_params
    bn, length = kv_cmp_rows.shape[:2]
    k_cmp = compress_rows(kv_cmp_rows[:, :, 0], pos_k, w1_k, b1_k, w2_k)
    v_cmp = compress_rows(kv_cmp_rows[:, :, 1], pos_v, w1_v, b1_v, w2_v)
    cmp_end = jnp.arange(k_cmp.shape[1]) * CMP_STRIDE + (CMP_BLOCK - 1)
    n_sel = -(-length // SEL_BLOCK)
    pad = n_sel * SEL_BLOCK - length
    sel = jnp.pad(kv_sel_rows, ((0, 0), (0, pad), (0, 0), (0, 0), (0, 0)))
    sel = sel.reshape(bn, n_sel, SEL_BLOCK, 2, N_KV_HEADS, HEAD_DIM)
    ks_blk = jnp.transpose(sel[:, :, :, 0], (0, 3, 1, 2, 4))
    vs_blk = jnp.transpose(sel[:, :, :, 1], (0, 3, 1, 2, 4))
    return k_cmp, v_cmp, cmp_end, ks_blk, vs_blk


def nsa_queries(q, gates, q_pos, ctx, kv_win, win_pos, rel_bias):
    k_cmp, v_cmp, cmp_end, ks_blk, vs_blk = ctx
    f32 = jnp.float32
    bn, tq = q.shape[:2]
    q = q * HEAD_DIM ** -0.5
    bias_tab = rel_bias.astype(f32).reshape(NUM_BUCKETS, N_KV_HEADS, GROUP)
    h_idx = jnp.arange(N_KV_HEADS)[None, None, :, None]
    dist_c = q_pos[:, None] - cmp_end[None, :]
    bias_c = jnp.transpose(bias_tab[t5_bucket(dist_c)], (0, 2, 3, 1))
    logit_c = jnp.einsum('bqhgd,bchd->bqhgc', q, k_cmp).astype(f32) + bias_c
    p_c = masked_softmax(logit_c, (dist_c >= 0)[:, None, None, :])
    o_c = jnp.einsum('bqhgc,bchd->bqhgd', p_c.astype(v_cmp.dtype), v_cmp)
    n_sel = ks_blk.shape[2]
    sel_start = jnp.arange(n_sel) * SEL_BLOCK
    cmp_start = cmp_end - (CMP_BLOCK - 1)
    cover = ((cmp_start[:, None] < sel_start[None, :] + SEL_BLOCK)
             & (cmp_end[:, None] >= sel_start[None, :])).astype(f32)
    importance = jnp.einsum('bqhgc,cs->bqhs', p_c, cover)
    blk_t = q_pos // SEL_BLOCK
    j = jnp.arange(n_sel)[None, :]
    forced = (j == 0) | (j == blk_t[:, None]) | (j == blk_t[:, None] - 1)
    causal = sel_start[None, :] <= q_pos[:, None]
    score = jnp.where(forced[:, None, :], BIG, jnp.where(causal[:, None, :], importance, -BIG))
    n_top = min(TOP_N, n_sel)
    _, sel = lax.top_k(score, n_top)
    b_idx = jnp.arange(bn)[:, None, None, None]
    n_keys = n_top * SEL_BLOCK
    ks_g = ks_blk[b_idx, h_idx, sel].reshape(bn, tq, N_KV_HEADS, n_keys, HEAD_DIM)
    vs_g = vs_blk[b_idx, h_idx, sel].reshape(bn, tq, N_KV_HEADS, n_keys, HEAD_DIM)
    kpos = (sel[..., None] * SEL_BLOCK + jnp.arange(SEL_BLOCK)).reshape(bn, tq, N_KV_HEADS, n_keys)
    dist_s = q_pos[None, :, None, None] - kpos
    bias_s = jnp.moveaxis(bias_tab[t5_bucket(dist_s), h_idx], -1, 3)
    logit_s = jnp.einsum('bqhgd,bqhkd->bqhgk', q, ks_g).astype(f32) + bias_s
    p_s = masked_softmax(logit_s, (dist_s >= 0)[:, :, :, None, :])
    o_s = jnp.einsum('bqhgk,bqhkd->bqhgd', p_s.astype(vs_g.dtype), vs_g)
    dist_w = q_pos[:, None] - win_pos[None, :]
    mask_w = (dist_w >= 0) & (dist_w <= WINDOW) & (win_pos[None, :] >= 0)
    bias_w = jnp.transpose(bias_tab[t5_bucket(dist_w)], (0, 2, 3, 1))
    logit_w = jnp.einsum('bqhgd,bkhd->bqhgk', q, kv_win[:, :, 0]).astype(f32) + bias_w
    p_w = masked_softmax(logit_w, mask_w[:, None, None, :])
    o_w = jnp.einsum('bqhgk,bkhd->bqhgd', p_w.astype(kv_win.dtype), kv_win[:, :, 1])
    return gates[..., 0:1] * o_c + gates[..., 1:2] * o_s + gates[..., 2:3] * o_w


def s5_branch(u, s0_re, s0_im, lam_re, lam_im, log_dt, b_re, b_im, c_re, c_im, d_skip, w_glu):
    f32 = jnp.float32
    bn, t_len, _ = u.shape
    uf = u.astype(f32)
    ug = uf.reshape(bn, t_len, N_SSM_GROUPS, SSM_GROUP_CH)
    dt = jnp.exp(log_dt.astype(f32))[:, None]
    lr, li = lam_re.astype(f32), lam_im.astype(f32)
    mag = jnp.exp(lr * dt)
    ab_re, ab_im = mag * jnp.cos(li * dt), mag * jnp.sin(li * dt)
    den = lr * lr + li * li
    f_re = ((ab_re - 1.0) * lr + ab_im * li) / den
    f_im = (ab_im * lr - (ab_re - 1.0) * li) / den
    br, bi = b_re.astype(f32), b_im.astype(f32)
    bb_re = f_re[:, :, None] * br - f_im[:, :, None] * bi
    bb_im = f_re[:, :, None] * bi + f_im[:, :, None] * br
    bu_re = jnp.einsum('btgc,gpc->btgp', ug, bb_re)
    bu_im = jnp.einsum('btgc,gpc->btgp', ug, bb_im)
    bu_re = bu_re.at[:, 0].add(ab_re * s0_re.astype(f32) - ab_im * s0_im.astype(f32))
    bu_im = bu_im.at[:, 0].add(ab_re * s0_im.astype(f32) + ab_im * s0_re.astype(f32))
    a_re = jnp.broadcast_to(ab_re, bu_re.shape)
    a_im = jnp.broadcast_to(ab_im, bu_im.shape)

    def combine(e1, e2):
        a1r, a1i, b1r, b1i = e1
        a2r, a2i, b2r, b2i = e2
        return (a1r * a2r - a1i * a2i, a1r * a2i + a1i * a2r,
                a2r * b1r - a2i * b1i + b2r, a2r * b1i + a2i * b1r + b2i)

    _, _, s_re, s_im = lax.associative_scan(combine, (a_re, a_im, bu_re, bu_im), axis=1)
    y = (jnp.einsum('btgp,gcp->btgc', s_re, c_re.astype(f32))
         - jnp.einsum('btgp,gcp->btgc', s_im, c_im.astype(f32)))
    y = jax.nn.gelu(y.reshape(bn, t_len, SSM_WIDTH) + d_skip.astype(f32) * uf)
    y = y * jax.nn.sigmoid(y @ w_glu.astype(f32))
    return y.astype(u.dtype), s_re[:, -1], s_im[:, -1]


def split_projection(h, w_in):
    bn, t_len, _ = h.shape
    q, kv_c, kv_s, kv_w, nsa_g, u, mg = jnp.split(h @ w_in, list(SPLIT_POINTS), axis=-1)
    q = q.reshape(bn, t_len, N_KV_HEADS, GROUP, HEAD_DIM)
    nsa_g = jax.nn.sigmoid(nsa_g).reshape(bn, t_len, N_KV_HEADS, GROUP, 3)
    kv_shape = (bn, t_len, 2, N_KV_HEADS, HEAD_DIM)
    return q, kv_c.reshape(kv_shape), kv_s.reshape(kv_shape), kv_w.reshape(kv_shape), nsa_g, u, mg


def merge_branches(attn_o, ssm_o, mg, w_branch_attn, w_branch_ssm, w_out):
    g_a, g_s = jnp.split(jax.nn.sigmoid(mg), 2, axis=-1)
    return (g_a * (attn_o @ w_branch_attn) + g_s * (ssm_o @ w_branch_ssm)) @ w_out


def mixer_prompt(h, w_in, cmp_params, rel_bias, ssm_params, w_branch_attn, w_branch_ssm, w_out):
    bn, t_len, _ = h.shape
    q, kv_c, kv_s, kv_w, nsa_g, u, mg = split_projection(h, w_in)
    ctx = nsa_context(kv_c, kv_s, cmp_params)
    n_blk = t_len // Q_BLOCK
    q_blk = jnp.moveaxis(q.reshape(bn, n_blk, Q_BLOCK, N_KV_HEADS, GROUP, HEAD_DIM), 1, 0)
    g_blk = jnp.moveaxis(nsa_g.reshape(bn, n_blk, Q_BLOCK, N_KV_HEADS, GROUP, 3), 1, 0)
    kv_w_pad = jnp.pad(kv_w, ((0, 0), (WINDOW, 0), (0, 0), (0, 0), (0, 0)))

    def one_block(args):
        blk, qb, gb = args
        start = blk * Q_BLOCK
        kvb = lax.dynamic_slice_in_dim(kv_w_pad, start, Q_BLOCK + WINDOW, axis=1)
        q_pos = start + jnp.arange(Q_BLOCK)
        win_pos = start - WINDOW + jnp.arange(Q_BLOCK + WINDOW)
        return nsa_queries(qb, gb, q_pos, ctx, kvb, win_pos, rel_bias)

    o = lax.map(one_block, (jnp.arange(n_blk), q_blk, g_blk))
    attn_o = jnp.moveaxis(o, 0, 1).reshape(bn, t_len, ATTN_WIDTH)
    s0 = jnp.zeros((bn, N_SSM_GROUPS, SSM_STATE), jnp.float32)
    ssm_o, s_re, s_im = s5_branch(u, s0, s0, *ssm_params)
    out = merge_branches(attn_o, ssm_o, mg, w_branch_attn, w_branch_ssm, w_out)
    win_keep = min(WINDOW, t_len)
    return out, kv_c, kv_s, kv_w[:, t_len - win_keep:], s_re, s_im


def mixer_sample(h, cache_kv_cmp, cache_kv_sel, state_kv_win, s0_re, s0_im, page_table,
                 w_in, cmp_params, rel_bias, ssm_params, w_branch_attn, w_branch_ssm, w_out):
    bn, s_len, _ = h.shape
    q, kv_c, kv_s, kv_w, nsa_g, u, mg = split_projection(h, w_in)
    past_len = page_table.shape[1] * PAGE_SIZE
    past_c = cache_kv_cmp[page_table].reshape(bn, past_len, 2, N_KV_HEADS, HEAD_DIM)
    past_s = cache_kv_sel[page_table].reshape(bn, past_len, 2, N_KV_HEADS, HEAD_DIM)
    ctx = nsa_context(jnp.concatenate([past_c, kv_c], axis=1),
                      jnp.concatenate([past_s, kv_s], axis=1), cmp_params)
    win_len = state_kv_win.shape[1]
    kv_w_all = jnp.concatenate([state_kv_win, kv_w], axis=1)
    q_pos = past_len + jnp.arange(s_len)
    win_pos = past_len - win_len + jnp.arange(win_len + s_len)
    o = nsa_queries(q, nsa_g, q_pos, ctx, kv_w_all, win_pos, rel_bias)
    attn_o = o.reshape(bn, s_len, ATTN_WIDTH)
    ssm_o, s_re, s_im = s5_branch(u, s0_re, s0_im, *ssm_params)
    out = merge_branches(attn_o, ssm_o, mg, w_branch_attn, w_branch_ssm, w_out)
    return out, kv_c, kv_s, kv_w_all[:, s_len:], s_re, s_im


def setup_inputs(seed: int = 0) -> dict:
    key = jax.random.key(seed)
    ks = iter(jax.random.split(key, 64))
    f32 = jnp.float32

    def nrm(shape, scale):
        return jax.random.normal(next(ks), shape, f32) * scale

    def gain(shape):
        return 1.0 + nrm(shape, 0.02)

    n_pages = PAST_LEN // PAGE_SIZE
    n_phys = (DEC_BATCH * n_pages * 5) // 4
    win_buf = min(WINDOW, PAST_LEN)
    page_table = jax.random.permutation(next(ks), n_phys)[: DEC_BATCH * n_pages]
    page_table = page_table.reshape(DEC_BATCH, n_pages).astype(jnp.int32)
    lam_im = jnp.pi * jnp.arange(SSM_STATE, dtype=f32)
    kv_row = (2, N_KV_HEADS, HEAD_DIM)
    return {
        'x_prompt': nrm((BATCH, SEQ, D_MODEL), 1.0),
        'x_sample': nrm((DEC_BATCH, DEC_SEQ, D_MODEL), 1.0),
        'cache_kv_cmp': nrm((DEPTH, n_phys, PAGE_SIZE) + kv_row, 1.0),
        'cache_kv_sel': nrm((DEPTH, n_phys, PAGE_SIZE) + kv_row, 1.0),
        'state_kv_win': nrm((DEPTH, DEC_BATCH, win_buf) + kv_row, 1.0),
        'state_ssm_re': nrm((DEPTH, DEC_BATCH, N_SSM_GROUPS, SSM_STATE), 0.5),
        'state_ssm_im': nrm((DEPTH, DEC_BATCH, N_SSM_GROUPS, SSM_STATE), 0.5),
        'page_table': page_table,
        'p_prompt': nrm((DEPTH, BATCH, SEQ, PLE_DIM), 1.0),
        'p_sample': nrm((DEPTH, DEC_BATCH, DEC_SEQ, PLE_DIM), 1.0),
        'rel_bias': nrm((NUM_BUCKETS, N_HEADS), 0.5),
        'g_ffn1': gain((DEPTH, D_MODEL)),
        'w_ffn1_gate': nrm((DEPTH, D_MODEL, D_FF), D_MODEL ** -0.5),
        'w_ffn1_up': nrm((DEPTH, D_MODEL, D_FF), D_MODEL ** -0.5),
        'w_ffn1_down': nrm((DEPTH, D_FF, D_MODEL), D_FF ** -0.5),
        'g_mix': gain((DEPTH, D_MODEL)),
        'w_in': nrm((DEPTH, D_MODEL, IN_WIDTH), D_MODEL ** -0.5),
        'cmp_pos_k': nrm((DEPTH, CMP_BLOCK, HEAD_DIM), 0.1),
        'cmp_w1_k': nrm((DEPTH, CMP_BLOCK * HEAD_DIM, CMP_HIDDEN), (CMP_BLOCK * HEAD_DIM) ** -0.5),
        'cmp_b1_k': nrm((DEPTH, CMP_HIDDEN), 0.01),
        'cmp_w2_k': nrm((DEPTH, CMP_HIDDEN, HEAD_DIM), CMP_HIDDEN ** -0.5),
        'cmp_pos_v': nrm((DEPTH, CMP_BLOCK, HEAD_DIM), 0.1),
        'cmp_w1_v': nrm((DEPTH, CMP_BLOCK * HEAD_DIM, CMP_HIDDEN), (CMP_BLOCK * HEAD_DIM) ** -0.5),
        'cmp_b1_v': nrm((DEPTH, CMP_HIDDEN), 0.01),
        'cmp_w2_v': nrm((DEPTH, CMP_HIDDEN, HEAD_DIM), CMP_HIDDEN ** -0.5),
        'ssm_lambda_re': -0.5 + nrm((DEPTH, N_SSM_GROUPS, SSM_STATE), 0.01),
        'ssm_lambda_im': lam_im + nrm((DEPTH, N_SSM_GROUPS, SSM_STATE), 0.01),
        'ssm_log_dt': jax.random.uniform(next(ks), (DEPTH, N_SSM_GROUPS), f32,
                                         minval=math.log(DT_MIN), maxval=math.log(DT_MAX)),
        'ssm_b_re': nrm((DEPTH, N_SSM_GROUPS, SSM_STATE, SSM_GROUP_CH), (2 * SSM_GROUP_CH) ** -0.5),
        'ssm_b_im': nrm((DEPTH, N_SSM_GROUPS, SSM_STATE, SSM_GROUP_CH), (2 * SSM_GROUP_CH) ** -0.5),
        'ssm_c_re': nrm((DEPTH, N_SSM_GROUPS, SSM_GROUP_CH, SSM_STATE), (2 * SSM_STATE) ** -0.5),
        'ssm_c_im': nrm((DEPTH, N_SSM_GROUPS, SSM_GROUP_CH, SSM_STATE), (2 * SSM_STATE) ** -0.5),
        'ssm_d': nrm((DEPTH, SSM_WIDTH), 1.0),
        'w_glu': nrm((DEPTH, SSM_WIDTH, SSM_WIDTH), SSM_WIDTH ** -0.5),
        'w_branch_attn': nrm((DEPTH, ATTN_WIDTH, D_MODEL), ATTN_WIDTH ** -0.5),
        'w_branch_ssm': nrm((DEPTH, SSM_WIDTH, D_MODEL), SSM_WIDTH ** -0.5),
        'w_out': nrm((DEPTH, D_MODEL, D_MODEL), D_MODEL ** -0.5),
        'g_ffn2': gain((DEPTH, D_MODEL)),
        'w_ffn2_gate': nrm((DEPTH, D_MODEL, D_FF), D_MODEL ** -0.5),
        'w_ffn2_up': nrm((DEPTH, D_MODEL, D_FF), D_MODEL ** -0.5),
        'w_ffn2_down': nrm((DEPTH, D_FF, D_MODEL), D_FF ** -0.5),
        'g_ple': gain((DEPTH, D_MODEL)),
        'w_ple_gate': nrm((DEPTH, D_MODEL, D_MODEL), D_MODEL ** -0.5),
        'w_ple_proj': nrm((DEPTH, PLE_DIM, D_MODEL), PLE_DIM ** -0.5),
        'g_final': gain((D_MODEL,)),
    }


def reference(x_prompt, x_sample, cache_kv_cmp, cache_kv_sel, state_kv_win, state_ssm_re, state_ssm_im,
              page_table, p_prompt, p_sample, rel_bias,
              g_ffn1, w_ffn1_gate, w_ffn1_up, w_ffn1_down, g_mix, w_in,
              cmp_pos_k, cmp_w1_k, cmp_b1_k, cmp_w2_k, cmp_pos_v, cmp_w1_v, cmp_b1_v, cmp_w2_v,
              ssm_lambda_re, ssm_lambda_im, ssm_log_dt, ssm_b_re, ssm_b_im, ssm_c_re, ssm_c_im, ssm_d, w_glu,
              w_branch_attn, w_branch_ssm, w_out, g_ffn2, w_ffn2_gate, w_ffn2_up, w_ffn2_down,
              g_ple, w_ple_gate, w_ple_proj, g_final):
    xp, xs = x_prompt, x_sample
    kvc_p, kvs_p, kvw_p, sre_p, sim_p = [], [], [], [], []
    kvc_s, kvs_s, kvw_s, sre_s, sim_s = [], [], [], [], []
    for i in range(DEPTH):
        ffn1 = (g_ffn1[i], w_ffn1_gate[i], w_ffn1_up[i], w_ffn1_down[i])
        ffn2 = (g_ffn2[i], w_ffn2_gate[i], w_ffn2_up[i], w_ffn2_down[i])
        cmp_params = (cmp_pos_k[i], cmp_w1_k[i], cmp_b1_k[i], cmp_w2_k[i],
                      cmp_pos_v[i], cmp_w1_v[i], cmp_b1_v[i], cmp_w2_v[i])
        ssm_params = (ssm_lambda_re[i], ssm_lambda_im[i], ssm_log_dt[i], ssm_b_re[i], ssm_b_im[i],
                      ssm_c_re[i], ssm_c_im[i], ssm_d[i], w_glu[i])
        xp = half_ffn(xp, *ffn1)
        xs = half_ffn(xs, *ffn1)
        mp, c_p, s_p, w_p, r_p, m_p = mixer_prompt(rmsnorm(xp, g_mix[i]), w_in[i], cmp_params, rel_bias,
                                                   ssm_params, w_branch_attn[i], w_branch_ssm[i], w_out[i])
        ms, c_s, s_s, w_s, r_s, m_s = mixer_sample(rmsnorm(xs, g_mix[i]), cache_kv_cmp[i], cache_kv_sel[i],
                                                   state_kv_win[i], state_ssm_re[i], state_ssm_im[i], page_table,
                                                   w_in[i], cmp_params, rel_bias, ssm_params,
                                                   w_branch_attn[i], w_branch_ssm[i], w_out[i])
        xp = xp + mp
        xs = xs + ms
        xp = half_ffn(xp, *ffn2)
        xs = half_ffn(xs, *ffn2)
        xp = ple_add(xp, p_prompt[i], g_ple[i], w_ple_gate[i], w_ple_proj[i])
        xs = ple_add(xs, p_sample[i], g_ple[i], w_ple_gate[i], w_ple_proj[i])
        kvc_p.append(c_p); kvs_p.append(s_p); kvw_p.append(w_p); sre_p.append(r_p); sim_p.append(m_p)
        kvc_s.append(c_s); kvs_s.append(s_s); kvw_s.append(w_s); sre_s.append(r_s); sim_s.append(m_s)
    y_prompt = rmsnorm(xp, g_final)
    y_sample = rmsnorm(xs, g_final)
    return (y_prompt, y_sample,
            jnp.stack(kvc_p), jnp.stack(kvs_p), jnp.stack(kvw_p), jnp.stack(sre_p), jnp.stack(sim_p),
            jnp.stack(kvc_s), jnp.stack(kvs_s), jnp.stack(kvw_s), jnp.stack(sre_s), jnp.stack(sim_s))
```

```python
import functools
import math

import numpy as np
import jax
import jax.numpy as jnp
from jax import lax
from jax.experimental import pallas as pl
from jax.experimental.pallas import tpu as pltpu

F32 = jnp.float32
BF16 = jnp.bfloat16

D_MODEL = 1024
N_HEADS = 8
N_KV_HEADS = 2
HEAD_DIM = 64
GROUP = N_HEADS // N_KV_HEADS
ATTN_WIDTH = N_HEADS * HEAD_DIM
KV_WIDTH = 2 * N_KV_HEADS * HEAD_DIM
CMP_BLOCK = 32
CMP_STRIDE = 16
CMP_HIDDEN = 128
SEL_BLOCK = 64
SEL_SHIFT = 6
TOP_N = 8
WINDOW = 512
Q_BLOCK = 128
NUM_BUCKETS = 32
MAX_EXACT = 16
MAX_DISTANCE = 128
SSM_WIDTH = 512
SSM_GROUP_CH = 16
N_SSM_GROUPS = SSM_WIDTH // SSM_GROUP_CH
SSM_STATE = 64
D_FF = 2816
PLE_DIM = 256
RMS_EPS = 1e-6
NEG_INF = -1e30
BIG = 1e9
PAGE_SIZE = 128

LANES = 128
SUBLANES = 8
HEAD_PAD = LANES
Q_PAD_WIDTH = N_HEADS * HEAD_PAD
GATE_PAD = LANES
SSM_LANES = N_SSM_GROUPS * SSM_STATE
SSM_HALF = SSM_LANES // 2
SCAN_LANES = 512
CHUNK_ROWS = CMP_STRIDE
CHUNK_LANES = CHUNK_ROWS * KV_WIDTH
FF_CHUNK = 256

TOKEN_TILE = 512
VMEM_LIMIT = 56 << 20


def _cparams(sem):
    return pltpu.CompilerParams(dimension_semantics=sem, vmem_limit_bytes=VMEM_LIMIT)


def _const_spec(shape):
    zeros = (0,) * len(shape)
    return pl.BlockSpec(shape, lambda *_: zeros, pipeline_mode=pl.Buffered(1))


def _dot(a, b):
    return jnp.dot(a, b, preferred_element_type=F32)


def _dot_nt(a, b, precision=None):
    return lax.dot_general(a, b, (((1,), (1,)), ((), ())), preferred_element_type=F32, precision=precision)


def _rms(x, g):
    var = jnp.mean(x * x, axis=-1, keepdims=True)
    return x * lax.rsqrt(var + RMS_EPS) * g


def _ffn_body(x_ref, g_ref, wg_ref, wu_ref, wd_ref, o_ref):
    x = x_ref[...]
    h = _rms(x, g_ref[...]).astype(BF16)
    acc = None
    for c in range(D_FF // FF_CHUNK):
        sl = slice(c * FF_CHUNK, (c + 1) * FF_CHUNK)
        gate = _dot(h, wg_ref[:, sl])
        up = _dot(h, wu_ref[:, sl])
        act = (gate * jax.nn.sigmoid(gate) * up).astype(BF16)
        part = _dot(act, wd_ref[sl, :])
        acc = part if acc is None else acc + part
    o_ref[...] = x + 0.5 * acc


def _half_ffn(x, g, wg, wu, wd):
    m = x.shape[0]
    tm = min(TOKEN_TILE, m)
    return pl.pallas_call(
        _ffn_body,
        out_shape=jax.ShapeDtypeStruct((m, D_MODEL), F32),
        grid=(m // tm,),
        in_specs=[pl.BlockSpec((tm, D_MODEL), lambda i: (i, 0)),
                  _const_spec((1, D_MODEL)),
                  _const_spec((D_MODEL, D_FF)), _const_spec((D_MODEL, D_FF)), _const_spec((D_FF, D_MODEL))],
        out_specs=pl.BlockSpec((tm, D_MODEL), lambda i: (i, 0)),
        compiler_params=_cparams(("parallel",)),
        name="half_ffn",
    )(x, g, wg, wu, wd)


_C_Q = 0
_C_KVC = _C_Q + Q_PAD_WIDTH
_C_KVS = _C_KVC + KV_WIDTH
_C_KVW = _C_KVS + KV_WIDTH
_C_U = _C_KVW + KV_WIDTH
_C_GATE = _C_U + SSM_WIDTH
_C_END = _C_GATE + GATE_PAD


def _inproj_body(x_ref, g_ref, w_ref, q_ref, kvc_ref, kvs_ref, kvw_ref, kvsb_ref, kvwb_ref, u_ref, gt_ref):
    h = _rms(x_ref[...], g_ref[...]).astype(BF16)
    q_ref[...] = _dot(h, w_ref[:, _C_Q:_C_KVC]).astype(BF16)
    kvc_ref[...] = _dot(h, w_ref[:, _C_KVC:_C_KVS])
    kvs = _dot(h, w_ref[:, _C_KVS:_C_KVW])
    kvs_ref[...] = kvs
    kvsb_ref[...] = kvs.astype(BF16)
    kvw = _dot(h, w_ref[:, _C_KVW:_C_U])
    kvw_ref[...] = kvw
    kvwb_ref[...] = kvw.astype(BF16)
    u_ref[...] = _dot(h, w_ref[:, _C_U:_C_GATE])
    gt_ref[...] = jax.nn.sigmoid(_dot(h, w_ref[:, _C_GATE:_C_END]))


def _in_projection(x, g, w_cat, n_batch, n_time):
    m = x.shape[0]
    tm = min(TOKEN_TILE, n_time)
    nt = n_time // tm
    row = lambda b, t: (b * nt + t, 0)
    row_spec = lambda w: pl.BlockSpec((tm, w), row)
    out_shape = (jax.ShapeDtypeStruct((m, Q_PAD_WIDTH), BF16),
                 jax.ShapeDtypeStruct((m, KV_WIDTH), F32), jax.ShapeDtypeStruct((m, KV_WIDTH), F32),
                 jax.ShapeDtypeStruct((m, KV_WIDTH), F32),
                 jax.ShapeDtypeStruct((m, KV_WIDTH), BF16), jax.ShapeDtypeStruct((m, KV_WIDTH), BF16),
                 jax.ShapeDtypeStruct((n_time, n_batch * SSM_WIDTH), F32),
                 jax.ShapeDtypeStruct((m, GATE_PAD), F32))
    out_specs = (row_spec(Q_PAD_WIDTH), row_spec(KV_WIDTH), row_spec(KV_WIDTH), row_spec(KV_WIDTH),
                 row_spec(KV_WIDTH), row_spec(KV_WIDTH),
                 pl.BlockSpec((tm, SSM_WIDTH), lambda b, t: (t, b)),
                 row_spec(GATE_PAD))
    return pl.pallas_call(
        _inproj_body,
        out_shape=out_shape,
        grid=(n_batch, nt),
        in_specs=[row_spec(D_MODEL), _const_spec((1, D_MODEL)), _const_spec((D_MODEL, _C_END))],
        out_specs=out_specs,
        compiler_params=_cparams(("parallel", "parallel")),
        name="in_projection",
    )(x, g, w_cat)


def _merge_body(x_ref, o_ref, s_ref, g_ref, wmg_ref, wa_ref, wb_ref, wo_ref, y_ref):
    x = x_ref[...]
    h = _rms(x, g_ref[...]).astype(BF16)
    mg = jax.nn.sigmoid(_dot(h, wmg_ref[...]))
    a = _dot(o_ref[...], wa_ref[...])
    s = _dot(s_ref[...].astype(BF16), wb_ref[...])
    mix = (mg[:, :D_MODEL] * a + mg[:, D_MODEL:] * s).astype(BF16)
    y_ref[...] = x + _dot(mix, wo_ref[...])


def _merge(x, o_pad, ssm_tm, g, w_mg, wa_pad, wb, wo, n_batch, n_time):
    m = x.shape[0]
    tm = min(TOKEN_TILE, n_time)
    nt = n_time // tm
    row = lambda b, t: (b * nt + t, 0)
    return pl.pallas_call(
        _merge_body,
        out_shape=jax.ShapeDtypeStruct((m, D_MODEL), F32),
        grid=(n_batch, nt),
        in_specs=[pl.BlockSpec((tm, D_MODEL), row), pl.BlockSpec((tm, Q_PAD_WIDTH), row),
                  pl.BlockSpec((tm, SSM_WIDTH), lambda b, t: (t, b)),
                  _const_spec((1, D_MODEL)), _const_spec((D_MODEL, 2 * D_MODEL)),
                  _const_spec((Q_PAD_WIDTH, D_MODEL)), _const_spec((SSM_WIDTH, D_MODEL)),
                  _const_spec((D_MODEL, D_MODEL))],
        out_specs=pl.BlockSpec((tm, D_MODEL), row),
        compiler_params=_cparams(("parallel", "parallel")),
        name="merge_branches",
    )(x, o_pad, ssm_tm, g, w_mg, wa_pad, wb, wo)


def _ple_body(x_ref, p_ref, g_ref, wg_ref, wp_ref, gf_ref, y_ref, *, final_norm):
    x = x_ref[...]
    h = _rms(x, g_ref[...]).astype(BF16)
    gate = jax.nn.sigmoid(_dot(h, wg_ref[...]))
    proj = _dot(p_ref[...].astype(BF16), wp_ref[...])
    y = x + gate * proj
    y_ref[...] = _rms(y, gf_ref[...]) if final_norm else y


def _ple_add(x, p, g, wg, wp, g_final, final_norm):
    m = x.shape[0]
    tm = min(TOKEN_TILE, m)
    return pl.pallas_call(
        functools.partial(_ple_body, final_norm=final_norm),
        out_shape=jax.ShapeDtypeStruct((m, D_MODEL), F32),
        grid=(m // tm,),
        in_specs=[pl.BlockSpec((tm, D_MODEL), lambda i: (i, 0)), pl.BlockSpec((tm, PLE_DIM), lambda i: (i, 0)),
                  _const_spec((1, D_MODEL)), _const_spec((D_MODEL, D_MODEL)), _const_spec((PLE_DIM, D_MODEL)),
                  _const_spec((1, D_MODEL))],
        out_specs=pl.BlockSpec((tm, D_MODEL), lambda i: (i, 0)),
        compiler_params=_cparams(("parallel",)),
        name="ple_final_norm",
    )(x, p, g, wg, wp, g_final)


def _compress_body(*refs, n_parts, n_prefetch):
    refs = refs[n_prefetch:]
    x_refs = refs[:n_parts]
    wab_ref, pos_ref, b1_ref, w2_ref, o_ref, bias_ref = refs[n_parts:]
    hid_w = 2 * N_KV_HEADS * CMP_HIDDEN

    @pl.when(pl.program_id(0) == 0)
    def _():
        pa = _dot(pos_ref[0:SUBLANES, :], wab_ref[:, :hid_w])
        pb = _dot(pos_ref[SUBLANES:, :], wab_ref[:, hid_w:])
        bias_ref[...] = pa + pb + b1_ref[...]

    parts = [r[0] for r in x_refs]
    x = (parts[0] if n_parts == 1 else jnp.concatenate(parts, axis=0)).astype(BF16)
    ab = _dot(x, wab_ref[...])
    n_chunks = x.shape[0]
    nxt = pltpu.roll(ab[:, hid_w:], n_chunks - 1, 0)
    hid = jax.nn.gelu(ab[:, :hid_w] + nxt + bias_ref[0:1, :])
    o_ref[0] = _dot(hid.astype(BF16), w2_ref[...]).astype(BF16)


def _compress(rows_chunked, page_table, w_ab, pos_ab, b1_big, w2_big):
    hid_w = 2 * N_KV_HEADS * CMP_HIDDEN
    w_specs = [_const_spec((CHUNK_LANES, 2 * hid_w)), _const_spec((2 * SUBLANES, CHUNK_LANES)),
               _const_spec((1, hid_w)), _const_spec((hid_w, KV_WIDTH))]
    scratch = [pltpu.VMEM((SUBLANES, hid_w), F32)]
    if page_table is None:
        n_seq, n_chunks, _ = rows_chunked.shape
        return pl.pallas_call(
            functools.partial(_compress_body, n_parts=1, n_prefetch=0),
            out_shape=jax.ShapeDtypeStruct((n_seq, n_chunks, KV_WIDTH), BF16),
            grid=(n_seq,),
            in_specs=[pl.BlockSpec((1, n_chunks, CHUNK_LANES), lambda b: (b, 0, 0))] + w_specs,
            out_specs=pl.BlockSpec((1, n_chunks, KV_WIDTH), lambda b: (b, 0, 0)),
            scratch_shapes=scratch,
            compiler_params=_cparams(("arbitrary",)),
            name="compress_prompt",
        )(rows_chunked, w_ab, pos_ab, b1_big, w2_big)
    n_seq, n_pages = page_table.shape
    page_chunks = rows_chunked.shape[1]
    n_chunks = n_pages * page_chunks
    page_specs = [pl.BlockSpec((1, page_chunks, CHUNK_LANES), (lambda b, pt, p=p: (pt[b, p], 0, 0)))
                  for p in range(n_pages)]
    const = lambda shape: pl.BlockSpec(shape, lambda b, pt: (0,) * len(shape), pipeline_mode=pl.Buffered(1))
    w_specs = [const((CHUNK_LANES, 2 * hid_w)), const((2 * SUBLANES, CHUNK_LANES)),
               const((1, hid_w)), const((hid_w, KV_WIDTH))]
    return pl.pallas_call(
        functools.partial(_compress_body, n_parts=n_pages, n_prefetch=1),
        out_shape=jax.ShapeDtypeStruct((n_seq, n_chunks, KV_WIDTH), BF16),
        grid_spec=pltpu.PrefetchScalarGridSpec(
            num_scalar_prefetch=1, grid=(n_seq,),
            in_specs=page_specs + w_specs,
            out_specs=pl.BlockSpec((1, n_chunks, KV_WIDTH), lambda b, pt: (b, 0, 0)),
            scratch_shapes=scratch),
        compiler_params=_cparams(("arbitrary",)),
        name="compress_sample",
    )(page_table, *([rows_chunked] * n_pages), w_ab, pos_ab, b1_big, w2_big)


def _ssm_body(u_ref, s0re_ref, s0im_ref, abre_ref, abim_ref, bre_ref, bim_ref, cre_ref, cim_ref, d_ref, wglu_ref,
              y_ref, sre_out, sim_out, bu_re, bu_im, st_re, st_im, *, t_chunk, n_batch):
    tc = pl.program_id(0)
    rows = t_chunk * n_batch

    @pl.when(tc == 0)
    def _():
        st_re[...] = s0re_ref[...]
        st_im[...] = s0im_ref[...]

    u = u_ref[...].reshape(rows, SSM_WIDTH)
    ub = u.astype(BF16)
    half_ch = SSM_WIDTH // 2
    for hf in range(2):
        uh = ub[:, hf * half_ch:(hf + 1) * half_ch]
        bu_re[:, hf * SSM_HALF:(hf + 1) * SSM_HALF] = _dot(uh, bre_ref[hf])
        bu_im[:, hf * SSM_HALF:(hf + 1) * SSM_HALF] = _dot(uh, bim_ref[hf])

    def scan_group(sg, carry):
        r0 = pl.multiple_of(sg * SUBLANES, SUBLANES)
        for lc in range(SSM_LANES // SCAN_LANES):
            ls = slice(lc * SCAN_LANES, (lc + 1) * SCAN_LANES)
            ar = jnp.broadcast_to(abre_ref[:, ls], (SUBLANES, SCAN_LANES))
            ai = jnp.broadcast_to(abim_ref[:, ls], (SUBLANES, SCAN_LANES))

            def step(t, st):
                sr, si = st
                row = pl.multiple_of(t * n_batch + r0, SUBLANES)
                nr = ar * sr - ai * si + bu_re[pl.ds(row, SUBLANES), ls]
                ni = ar * si + ai * sr + bu_im[pl.ds(row, SUBLANES), ls]
                bu_re[pl.ds(row, SUBLANES), ls] = nr
                bu_im[pl.ds(row, SUBLANES), ls] = ni
                return nr, ni

            sr, si = lax.fori_loop(0, t_chunk, step,
                                   (st_re[pl.ds(r0, SUBLANES), ls], st_im[pl.ds(r0, SUBLANES), ls]),
                                   unroll=min(t_chunk, 8))
            st_re[pl.ds(r0, SUBLANES), ls] = sr
            st_im[pl.ds(r0, SUBLANES), ls] = si
        return carry

    lax.fori_loop(0, n_batch // SUBLANES, scan_group, 0)

    ys = []
    for hf in range(2):
        ls = slice(hf * SSM_HALF, (hf + 1) * SSM_HALF)
        ys.append(_dot(bu_re[:, ls].astype(BF16), cre_ref[hf]) - _dot(bu_im[:, ls].astype(BF16), cim_ref[hf]))
    y = jax.nn.gelu(jnp.concatenate(ys, axis=1) + d_ref[...] * u)
    y = y * jax.nn.sigmoid(_dot(y.astype(BF16), wglu_ref[...]))
    y_ref[...] = y.reshape(t_chunk, n_batch, SSM_WIDTH)

    @pl.when(tc == pl.num_programs(0) - 1)
    def _():
        sre_out[...] = st_re[...]
        sim_out[...] = st_im[...]


def _ssm(u_tm, s0_re, s0_im, sp, t_chunk):
    n_time, n_batch, _ = u_tm.shape
    rows = t_chunk * n_batch
    st_spec = pl.BlockSpec((n_batch, SSM_LANES), lambda t: (0, 0))
    return pl.pallas_call(
        functools.partial(_ssm_body, t_chunk=t_chunk, n_batch=n_batch),
        out_shape=(jax.ShapeDtypeStruct((n_time, n_batch, SSM_WIDTH), F32),
                   jax.ShapeDtypeStruct((n_batch, SSM_LANES), F32), jax.ShapeDtypeStruct((n_batch, SSM_LANES), F32)),
        grid=(n_time // t_chunk,),
        in_specs=[pl.BlockSpec((t_chunk, n_batch, SSM_WIDTH), lambda t: (t, 0, 0)), st_spec, st_spec,
                  _const_spec((1, SSM_LANES)), _const_spec((1, SSM_LANES)),
                  _const_spec((2, SSM_WIDTH // 2, SSM_HALF)), _const_spec((2, SSM_WIDTH // 2, SSM_HALF)),
                  _const_spec((2, SSM_HALF, SSM_WIDTH // 2)), _const_spec((2, SSM_HALF, SSM_WIDTH // 2)),
                  _const_spec((1, SSM_WIDTH)), _const_spec((SSM_WIDTH, SSM_WIDTH))],
        out_specs=(pl.BlockSpec((t_chunk, n_batch, SSM_WIDTH), lambda t: (t, 0, 0)), st_spec, st_spec),
        scratch_shapes=[pltpu.VMEM((rows, SSM_LANES), F32), pltpu.VMEM((rows, SSM_LANES), F32),
                        pltpu.VMEM((n_batch, SSM_LANES), F32), pltpu.VMEM((n_batch, SSM_LANES), F32)],
        compiler_params=_cparams(("arbitrary",)),
        name="s5_scan",
    )(u_tm, s0_re, s0_im, sp["ab_re"], sp["ab_im"], sp["b_re"], sp["b_im"], sp["c_re"], sp["c_im"], sp["d"], sp["w_glu"])


def _masked_softmax(s, valid):
    mx = jnp.max(s, axis=1, keepdims=True)
    e = jnp.where(valid, jnp.exp(s - mx), 0.0)
    den = jnp.sum(e, axis=1, keepdims=True)
    return e / jnp.where(den > 0.0, den, 1.0)


def _topk_mask(score, idx, n_cand, axis):
    rank = jnp.zeros(score.shape, jnp.int32)
    for sp in range(n_cand):
        other = score[sp:sp + 1, :] if axis == 0 else score[:, sp:sp + 1]
        beats = (other > score) | ((other == score) & (sp < idx))
        rank = rank + beats.astype(jnp.int32)
    return (rank < TOP_N).astype(F32)


def _pattn_body(q_ref, gt_ref, cmp_ref, ks_ref, kw_ref, bc_ref, nt_ref, e3_ref, cov_ref, o_ref,
                qs, m_s, l_s, acc_s, oc_s, os_s):
    qb = pl.program_id(1)
    n_rows = GROUP * Q_BLOCK
    n_sel = e3_ref.shape[1]

    def flash_init():
        m_s[...] = jnp.full(m_s.shape, NEG_INF, F32)
        l_s[...] = jnp.zeros(l_s.shape, F32)
        acc_s[...] = jnp.zeros(acc_s.shape, F32)

    def flash_tile(kv_ref, kb, bias):
        kv = kv_ref[0, pl.ds(pl.multiple_of(kb * Q_BLOCK, Q_BLOCK), Q_BLOCK), :]
        s = _dot_nt(qs[...], kv[:, :LANES])
        if bias is not None:
            s = s + bias
        m_old = m_s[...]
        m_new = jnp.maximum(m_old, jnp.max(s, axis=1, keepdims=True))
        alpha = jnp.exp(m_old - m_new)
        p = jnp.exp(s - m_new)
        l_s[...] = alpha * l_s[...] + jnp.sum(p, axis=1, keepdims=True)
        acc_s[...] = alpha * acc_s[...] + _dot(p.astype(BF16), kv[:, LANES:])
        m_s[...] = m_new

    for h in range(N_KV_HEADS):
        for g in range(GROUP):
            hg = h * GROUP + g
            qs[g * Q_BLOCK:(g + 1) * Q_BLOCK, :] = q_ref[:, hg * HEAD_PAD:(hg + 1) * HEAD_PAD]
        near = nt_ref[h]

        kc = cmp_ref[0]
        bc = bc_ref[0, h]
        p = _masked_softmax(_dot_nt(qs[...], kc[:, :LANES]) + bc, bc > 0.5 * NEG_INF)
        oc_s[...] = _dot(p.astype(BF16), kc[:, LANES:])

        p4 = p[0:Q_BLOCK]
        for g in range(1, GROUP):
            p4 = p4 + p[g * Q_BLOCK:(g + 1) * Q_BLOCK]
        imp_t = _dot_nt(cov_ref[...], p4, precision=lax.Precision.HIGHEST)
        sidx = lax.broadcasted_iota(jnp.int32, (n_sel, Q_BLOCK), 0)
        qidx = lax.broadcasted_iota(jnp.int32, (n_sel, Q_BLOCK), 1)
        blk = (Q_BLOCK // SEL_BLOCK) * qb + jnp.right_shift(qidx, SEL_SHIFT)
        forced = (sidx == 0) | (sidx == blk) | (sidx == blk - 1)
        score = jnp.where(forced, BIG, jnp.where(sidx <= blk, imp_t, -BIG))
        sel = _topk_mask(score, sidx, n_sel, 0).T
        sel4 = jnp.concatenate([sel] * GROUP, axis=0).astype(BF16)

        def block_mask(kb):
            return (_dot(sel4, e3_ref[kb]) - 1.0) * (-NEG_INF)

        flash_init()
        flash_tile(ks_ref, qb, block_mask(qb) + near[:, Q_BLOCK:])

        @pl.when(qb >= 1)
        def _():
            flash_tile(ks_ref, qb - 1, block_mask(qb - 1) + near[:, :Q_BLOCK])

        def far_tile(kb, carry):
            flash_tile(ks_ref, kb, block_mask(kb))
            return carry

        lax.fori_loop(0, jnp.maximum(qb - 1, 0), far_tile, 0)
        os_s[...] = acc_s[...] / l_s[...]

        flash_init()
        flash_tile(kw_ref, qb, near[:, Q_BLOCK:])

        @pl.when(qb >= 1)
        def _():
            flash_tile(kw_ref, qb - 1, near[:, :Q_BLOCK])

        for back in range(2, WINDOW // Q_BLOCK):
            @pl.when(qb >= back)
            def _(back=back):
                flash_tile(kw_ref, qb - back, None)

        @pl.when(qb >= WINDOW // Q_BLOCK)
        def _():
            ri = lax.broadcasted_iota(jnp.int32, (n_rows, Q_BLOCK), 0) & (Q_BLOCK - 1)
            cj = lax.broadcasted_iota(jnp.int32, (n_rows, Q_BLOCK), 1)
            flash_tile(kw_ref, qb - WINDOW // Q_BLOCK, jnp.where(cj >= ri, 0.0, NEG_INF))

        ow = acc_s[...] / l_s[...]
        gates = gt_ref[...]
        for g in range(GROUP):
            hg = h * GROUP + g
            r = slice(g * Q_BLOCK, (g + 1) * Q_BLOCK)
            o = (gates[:, 3 * hg:3 * hg + 1] * oc_s[r, :] + gates[:, 3 * hg + 1:3 * hg + 2] * os_s[r, :]
                 + gates[:, 3 * hg + 2:3 * hg + 3] * ow[r, :])
            o_ref[:, hg * HEAD_PAD:(hg + 1) * HEAD_PAD] = o.astype(BF16)


def _prompt_attention(q_pad, gates, cmp_kv, kvs_b, kvw_b, tabs, n_batch, n_time):
    nq = n_time // Q_BLOCK
    n_rows = GROUP * Q_BLOCK
    row = lambda b, q: (b * nq + q, 0)
    seq = lambda b, q: (b, 0, 0)
    n_cmp_pad = cmp_kv.shape[1]
    return pl.pallas_call(
        _pattn_body,
        out_shape=jax.ShapeDtypeStruct((n_batch * n_time, Q_PAD_WIDTH), BF16),
        grid=(n_batch, nq),
        in_specs=[pl.BlockSpec((Q_BLOCK, Q_PAD_WIDTH), row), pl.BlockSpec((Q_BLOCK, GATE_PAD), row),
                  pl.BlockSpec((1, n_cmp_pad, KV_WIDTH), seq),
                  pl.BlockSpec((1, n_time, KV_WIDTH), seq), pl.BlockSpec((1, n_time, KV_WIDTH), seq),
                  pl.BlockSpec((1, N_KV_HEADS, n_rows, n_cmp_pad), lambda b, q: (q, 0, 0, 0)),
                  _const_spec(tabs["near"].shape), _const_spec(tabs["expand"].shape), _const_spec(tabs["cover_t"].shape)],
        out_specs=pl.BlockSpec((Q_BLOCK, Q_PAD_WIDTH), row),
        scratch_shapes=[pltpu.VMEM((n_rows, LANES), BF16),
                        pltpu.VMEM((n_rows, 1), F32), pltpu.VMEM((n_rows, 1), F32), pltpu.VMEM((n_rows, LANES), F32),
                        pltpu.VMEM((n_rows, LANES), F32), pltpu.VMEM((n_rows, LANES), F32)],
        compiler_params=_cparams(("parallel", "arbitrary")),
        name="prompt_attention",
    )(q_pad, gates, cmp_kv, kvs_b.reshape(n_batch, n_time, KV_WIDTH), kvw_b.reshape(n_batch, n_time, KV_WIDTH),
      tabs["bias_c"], tabs["near"], tabs["expand"], tabs["cover_t"])


def _sattn_body(*refs, n_pages, past_len):
    pt_ref = refs[0]
    del pt_ref
    q_ref, gt_ref, cmp_ref = refs[1:4]
    page_refs = refs[4:4 + n_pages]
    (ksn_ref, win_ref, kwn_ref, bc_ref, bs_ref, bw_ref, cov_ref, exp_ref, o_ref, kbuf, wbuf) = refs[4 + n_pages:]
    tq = q_ref.shape[1]
    n_rows = N_HEADS * tq
    win_len = win_ref.shape[1]

    for p in range(n_pages):
        kbuf[p * PAGE_SIZE:(p + 1) * PAGE_SIZE, :] = page_refs[p][0].astype(BF16)
    kbuf[past_len:past_len + tq, :] = ksn_ref[0].astype(BF16)
    kbuf[past_len + tq:, :] = jnp.zeros((kbuf.shape[0] - past_len - tq, KV_WIDTH), BF16)
    wbuf[0:win_len, :] = win_ref[0].astype(BF16)
    wbuf[win_len:win_len + tq, :] = kwn_ref[0].astype(BF16)
    wbuf[win_len + tq:, :] = jnp.zeros((wbuf.shape[0] - win_len - tq, KV_WIDTH), BF16)

    q = jnp.concatenate([q_ref[0, :, hg * HEAD_PAD:(hg + 1) * HEAD_PAD] for hg in range(N_HEADS)], axis=0)

    kc = cmp_ref[0]
    bc = bc_ref[...]
    p = _masked_softmax(_dot_nt(q, kc[:, :LANES]) + bc, bc > 0.5 * NEG_INF)
    oc = _dot(p.astype(BF16), kc[:, LANES:])

    sidx = lax.broadcasted_iota(jnp.int32, (tq, LANES), 1)
    tidx = lax.broadcasted_iota(jnp.int32, (tq, LANES), 0)
    blk = jnp.right_shift(past_len + tidx, SEL_SHIFT)
    forced = (sidx == 0) | (sidx == blk) | (sidx == blk - 1)
    n_sel = exp_ref.shape[1] // SEL_BLOCK
    sel_rows = []
    for h in range(N_KV_HEADS):
        base = h * GROUP * tq
        p4 = p[base:base + tq]
        for g in range(1, GROUP):
            p4 = p4 + p[base + g * tq:base + (g + 1) * tq]
        imp = jnp.dot(p4, cov_ref[...], preferred_element_type=F32, precision=lax.Precision.HIGHEST)
        score = jnp.where(forced, BIG, jnp.where(sidx <= blk, imp, -BIG))
        sel = _topk_mask(score, sidx, n_sel, 1)
        sel_rows.extend([sel] * GROUP)
    sel_all = jnp.concatenate(sel_rows, axis=0).astype(BF16)
    block_mask = (_dot(sel_all, exp_ref[...]) - 1.0) * (-NEG_INF)

    kv = kbuf[...]
    bs = bs_ref[...]
    ps = _masked_softmax(_dot_nt(q, kv[:, :LANES]) + bs + block_mask, bs > 0.5 * NEG_INF)
    osel = _dot(ps.astype(BF16), kv[:, LANES:])

    kw = wbuf[...]
    bw = bw_ref[...]
    pw = _masked_softmax(_dot_nt(q, kw[:, :LANES]) + bw, bw > 0.5 * NEG_INF)
    ow = _dot(pw.astype(BF16), kw[:, LANES:])

    gates = gt_ref[0]
    for hg in range(N_HEADS):
        r = slice(hg * tq, (hg + 1) * tq)
        o = (gates[:, 3 * hg:3 * hg + 1] * oc[r] + gates[:, 3 * hg + 1:3 * hg + 2] * osel[r]
             + gates[:, 3 * hg + 2:3 * hg + 3] * ow[r])
        o_ref[0, :, hg * HEAD_PAD:(hg + 1) * HEAD_PAD] = o.astype(BF16)
    del n_rows


def _sample_attention(q8, gates8, cmp_kv, cache_sel, page_table, kvs_new8, win_state, kvw_new8, tabs):
    n_seq, tq, _ = q8.shape
    n_pages = page_table.shape[1]
    past_len = n_pages * PAGE_SIZE
    win_len = win_state.shape[1]
    n_rows = N_HEADS * tq
    sel_keys = tabs["bias_s"].shape[1]
    win_keys = tabs["bias_w"].shape[1]
    seq = lambda b, pt: (b, 0, 0)
    const = lambda shape: pl.BlockSpec(shape, lambda b, pt: (0,) * len(shape), pipeline_mode=pl.Buffered(1))
    page_specs = [pl.BlockSpec((1, PAGE_SIZE, KV_WIDTH), (lambda b, pt, p=p: (pt[b, p], 0, 0))) for p in range(n_pages)]
    in_specs = ([pl.BlockSpec((1, tq, Q_PAD_WIDTH), seq), pl.BlockSpec((1, tq, GATE_PAD), seq),
                 pl.BlockSpec((1, cmp_kv.shape[1], KV_WIDTH), seq)] + page_specs +
                [pl.BlockSpec((1, tq, KV_WIDTH), seq), pl.BlockSpec((1, win_len, KV_WIDTH), seq),
                 pl.BlockSpec((1, tq, KV_WIDTH), seq),
                 const((n_rows, cmp_kv.shape[1])), const((n_rows, sel_keys)), const((n_rows, win_keys)),
                 const(tabs["cover"].shape), const(tabs["expand"].shape)])
    return pl.pallas_call(
        functools.partial(_sattn_body, n_pages=n_pages, past_len=past_len),
        out_shape=jax.ShapeDtypeStruct((n_seq, tq, Q_PAD_WIDTH), BF16),
        grid_spec=pltpu.PrefetchScalarGridSpec(
            num_scalar_prefetch=1, grid=(n_seq,), in_specs=in_specs,
            out_specs=pl.BlockSpec((1, tq, Q_PAD_WIDTH), seq),
            scratch_shapes=[pltpu.VMEM((sel_keys, KV_WIDTH), BF16), pltpu.VMEM((win_keys, KV_WIDTH), BF16)]),
        compiler_params=_cparams(("parallel",)),
        name="sample_attention",
    )(page_table, q8, gates8, cmp_kv, *([cache_sel] * n_pages), kvs_new8, win_state, kvw_new8,
      tabs["bias_c"], tabs["bias_s"], tabs["bias_w"], tabs["cover"], tabs["expand"])


def _t5_bucket_np(dist):
    n = np.maximum(dist, 0)
    nf = np.maximum(n, 1).astype(np.float64)
    large = MAX_EXACT + (np.log(nf / MAX_EXACT) / math.log(MAX_DISTANCE / MAX_EXACT)
                         * (NUM_BUCKETS - MAX_EXACT)).astype(np.int64)
    large = np.minimum(large, NUM_BUCKETS - 1)
    return np.where(n < MAX_EXACT, n, large).astype(np.int32)


def _bias_rows(rel_bias, dist, valid, far_shift):
    tab = rel_bias.astype(F32)
    b = tab[_t5_bucket_np(dist)]
    if far_shift:
        b = b - tab[NUM_BUCKETS - 1][None, None, :]
    b = jnp.where(jnp.asarray(valid)[:, :, None], b, NEG_INF)
    nq, nk = dist.shape
    return jnp.transpose(b, (2, 0, 1)).reshape(N_KV_HEADS, GROUP * nq, nk)


def _cover_np(n_cmp_pad, n_cmp, n_sel):
    c = np.arange(n_cmp_pad)
    start, end = c * CMP_STRIDE, c * CMP_STRIDE + CMP_BLOCK - 1
    s0 = np.arange(n_sel) * SEL_BLOCK
    cov = (start[:, None] < s0[None, :] + SEL_BLOCK) & (end[:, None] >= s0[None, :]) & (c[:, None] < n_cmp)
    return cov.astype(np.float32)


def _prompt_tables(rel_bias, n_time):
    nq = n_time // Q_BLOCK
    n_cmp = (n_time - CMP_BLOCK) // CMP_STRIDE + 1
    n_cmp_pad = n_time // CMP_STRIDE
    n_sel = n_time // SEL_BLOCK
    i = np.arange(Q_BLOCK)
    cmp_end = np.arange(n_cmp_pad) * CMP_STRIDE + CMP_BLOCK - 1
    bias_c = []
    for qb in range(nq):
        d = (qb * Q_BLOCK + i)[:, None] - cmp_end[None, :]
        bias_c.append(_bias_rows(rel_bias, d, (d >= 0) & (np.arange(n_cmp_pad) < n_cmp)[None, :], False))
    d = Q_BLOCK + i[:, None] - np.arange(2 * Q_BLOCK)[None, :]
    near = _bias_rows(rel_bias, d, d >= 0, True)
    expand = np.zeros((nq, n_sel, Q_BLOCK), np.float32)
    for kb in range(nq):
        for j in range(Q_BLOCK):
            expand[kb, (kb * Q_BLOCK + j) // SEL_BLOCK, j] = 1.0
    return {"bias_c": jnp.stack(bias_c), "near": near, "expand": jnp.asarray(expand, BF16),
            "cover_t": jnp.asarray(_cover_np(n_cmp_pad, n_cmp, n_sel).T)}


def _sample_tables(rel_bias, past_len, n_new, tq, win_len):
    n_cmp = (past_len + n_new - CMP_BLOCK) // CMP_STRIDE + 1
    n_cmp_pad = past_len // CMP_STRIDE
    assert n_cmp <= n_cmp_pad
    n_sel = -(-(past_len + n_new) // SEL_BLOCK)
    assert n_sel <= LANES
    sel_keys = -(-(past_len + tq) // LANES) * LANES
    win_keys = -(-(win_len + tq) // LANES) * LANES
    q_pos = past_len + np.minimum(np.arange(tq), n_new - 1)
    cmp_end = np.arange(n_cmp_pad) * CMP_STRIDE + CMP_BLOCK - 1
    d = q_pos[:, None] - cmp_end[None, :]
    bias_c = _bias_rows(rel_bias, d, (d >= 0) & (np.arange(n_cmp_pad) < n_cmp)[None, :], False)
    kpos = np.arange(sel_keys)
    d = q_pos[:, None] - kpos[None, :]
    bias_s = _bias_rows(rel_bias, d, (d >= 0) & (kpos < past_len + n_new)[None, :], False)
    j = np.arange(win_keys)
    wpos = np.where(j < win_len, past_len - win_len + j, past_len + j - win_len)
    d = q_pos[:, None] - wpos[None, :]
    bias_w = _bias_rows(rel_bias, d, (d >= 0) & (d <= WINDOW) & (wpos >= 0)[None, :] & (j < win_len + n_new)[None, :], False)
    cover = np.zeros((n_cmp_pad, LANES), np.float32)
    cover[:, :n_sel] = _cover_np(n_cmp_pad, n_cmp, n_sel)
    expand = np.zeros((LANES, sel_keys), np.float32)
    expand[kpos // SEL_BLOCK, kpos] = 1.0
    flat = lambda t: t.reshape(N_HEADS * tq, t.shape[-1])
    return {"bias_c": flat(bias_c), "bias_s": flat(bias_s), "bias_w": flat(bias_w),
            "cover": jnp.asarray(cover), "expand": jnp.asarray(expand, BF16)}


def _layer_params(w_in, cmp_params, ssm_params, w_branch_attn):
    q_w = w_in[:, :ATTN_WIDTH] * HEAD_DIM ** -0.5
    q_pad = jnp.zeros((D_MODEL, N_HEADS, HEAD_PAD), F32)
    wa_pad = jnp.zeros((N_HEADS, HEAD_PAD, D_MODEL), F32)
    for hg in range(N_HEADS):
        off = (hg // GROUP) * HEAD_DIM
        q_pad = q_pad.at[:, hg, off:off + HEAD_DIM].set(q_w[:, hg * HEAD_DIM:(hg + 1) * HEAD_DIM])
        wa_pad = wa_pad.at[hg, off:off + HEAD_DIM].set(w_branch_attn[hg * HEAD_DIM:(hg + 1) * HEAD_DIM])
    kv0 = ATTN_WIDTH
    g0 = kv0 + 3 * KV_WIDTH
    u0 = g0 + 3 * N_HEADS
    m0 = u0 + SSM_WIDTH
    gate_w = jnp.zeros((D_MODEL, GATE_PAD), F32).at[:, :3 * N_HEADS].set(w_in[:, g0:u0])
    w_cat = jnp.concatenate([q_pad.reshape(D_MODEL, Q_PAD_WIDTH), w_in[:, kv0:g0], w_in[:, u0:m0], gate_w], axis=1)

    pos_k, w1_k, b1_k, w2_k, pos_v, w1_v, b1_v, w2_v = cmp_params
    n_combo = 2 * N_KV_HEADS
    hid_w = n_combo * CMP_HIDDEN
    w_a = jnp.zeros((CHUNK_ROWS, n_combo, HEAD_DIM, n_combo, CMP_HIDDEN), F32)
    w_b = jnp.zeros_like(w_a)
    w2_big = jnp.zeros((n_combo, CMP_HIDDEN, n_combo, HEAD_DIM), F32)
    pos_a = jnp.zeros((CHUNK_ROWS, n_combo, HEAD_DIM), F32)
    pos_b = jnp.zeros_like(pos_a)
    for c in range(n_combo):
        w1, w2, pos = (w1_k, w2_k, pos_k) if c < N_KV_HEADS else (w1_v, w2_v, pos_v)
        w1r = w1.reshape(CMP_BLOCK, HEAD_DIM, CMP_HIDDEN)
        w_a = w_a.at[:, c, :, c, :].set(w1r[:CHUNK_ROWS])
        w_b = w_b.at[:, c, :, c, :].set(w1r[CHUNK_ROWS:])
        w2_big = w2_big.at[c, :, c, :].set(w2)
        pos_a = pos_a.at[:, c, :].set(pos[:CHUNK_ROWS])
        pos_b = pos_b.at[:, c, :].set(pos[CHUNK_ROWS:])
    w_ab = jnp.concatenate([w_a.reshape(CHUNK_LANES, hid_w), w_b.reshape(CHUNK_LANES, hid_w)], axis=1)
    pos_ab = jnp.concatenate([jnp.broadcast_to(pos_a.reshape(1, CHUNK_LANES), (SUBLANES, CHUNK_LANES)),
                              jnp.broadcast_to(pos_b.reshape(1, CHUNK_LANES), (SUBLANES, CHUNK_LANES))], axis=0)
    b1_big = jnp.concatenate([b1_k, b1_k, b1_v, b1_v]).reshape(1, hid_w).astype(F32)

    lam_re, lam_im, log_dt, b_re, b_im, c_re, c_im, d_skip, w_glu = ssm_params
    dt = jnp.exp(log_dt.astype(F32))[:, None]
    lr, li = lam_re.astype(F32), lam_im.astype(F32)
    mag = jnp.exp(lr * dt)
    ab_re, ab_im = mag * jnp.cos(li * dt), mag * jnp.sin(li * dt)
    den = lr * lr + li * li
    f_re = ((ab_re - 1.0) * lr + ab_im * li) / den
    f_im = (ab_im * lr - (ab_re - 1.0) * li) / den
    br, bi = b_re.astype(F32), b_im.astype(F32)
    bb_re = f_re[:, :, None] * br - f_im[:, :, None] * bi
    bb_im = f_re[:, :, None] * bi + f_im[:, :, None] * br
    half_g = N_SSM_GROUPS // 2

    def in_blocks(bb):
        out = jnp.zeros((2, half_g, SSM_GROUP_CH, half_g, SSM_STATE), F32)
        bt = jnp.transpose(bb, (0, 2, 1)).reshape(2, half_g, SSM_GROUP_CH, SSM_STATE)
        for g in range(half_g):
            out = out.at[:, g, :, g, :].set(bt[:, g])
        return out.reshape(2, SSM_WIDTH // 2, SSM_HALF).astype(BF16)

    def out_blocks(cc):
        out = jnp.zeros((2, half_g, SSM_STATE, half_g, SSM_GROUP_CH), F32)
        ct = jnp.transpose(cc.astype(F32), (0, 2, 1)).reshape(2, half_g, SSM_STATE, SSM_GROUP_CH)
        for g in range(half_g):
            out = out.at[:, g, :, g, :].set(ct[:, g])
        return out.reshape(2, SSM_HALF, SSM_WIDTH // 2).astype(BF16)

    ssm = {"ab_re": ab_re.reshape(1, SSM_LANES), "ab_im": ab_im.reshape(1, SSM_LANES),
           "b_re": in_blocks(bb_re), "b_im": in_blocks(bb_im), "c_re": out_blocks(c_re), "c_im": out_blocks(c_im),
           "d": d_skip.astype(F32).reshape(1, SSM_WIDTH), "w_glu": w_glu.astype(BF16)}
    return {"w_cat": w_cat.astype(BF16), "w_mg": w_in[:, m0:].astype(BF16),
            "wa_pad": wa_pad.reshape(Q_PAD_WIDTH, D_MODEL).astype(BF16),
            "w_ab": w_ab.astype(BF16), "pos_ab": pos_ab.astype(BF16), "b1_big": b1_big, "w2_big": w2_big.reshape(hid_w, KV_WIDTH).astype(BF16),
            "ssm": ssm}


def _row(v):
    return v.astype(F32).reshape(1, -1)


def _mixer_prompt(x, g_mix, lp, tabs, w_branch_ssm, w_out, n_batch, n_time, t_chunk=64):
    q_pad, kvc, kvs, kvw, kvs_b, kvw_b, u_tm, gates = _in_projection(x, g_mix, lp["w_cat"], n_batch, n_time)
    cmp_kv = _compress(kvc.reshape(n_batch, n_time // CHUNK_ROWS, CHUNK_LANES), None,
                       lp["w_ab"], lp["pos_ab"], lp["b1_big"], lp["w2_big"])
    o_pad = _prompt_attention(q_pad, gates, cmp_kv, kvs_b, kvw_b, tabs, n_batch, n_time)
    s0 = jnp.zeros((n_batch, SSM_LANES), F32)
    y_tm, s_re, s_im = _ssm(u_tm.reshape(n_time, n_batch, SSM_WIDTH), s0, s0, lp["ssm"], t_chunk)
    x2 = _merge(x, o_pad, y_tm.reshape(n_time, n_batch * SSM_WIDTH), g_mix, lp["w_mg"], lp["wa_pad"],
                w_branch_ssm, w_out, n_batch, n_time)
    return x2, kvc, kvs, kvw, s_re, s_im


def _mixer_sample(x, g_mix, lp, tabs, w_branch_ssm, w_out, cache_cmp, cache_sel, state_win, s0_re, s0_im, page_table,
                  n_seq, n_new, tq):
    m = n_seq * n_new
    q_pad, kvc, kvs, kvw, _, _, u, gates = _in_projection(x, g_mix, lp["w_cat"], 1, m)
    n_phys = cache_cmp.shape[0]
    cmp_kv = _compress(cache_cmp.reshape(n_phys, PAGE_SIZE // CHUNK_ROWS, CHUNK_LANES), page_table,
                       lp["w_ab"], lp["pos_ab"], lp["b1_big"], lp["w2_big"])
    pad_t = lambda a: jnp.pad(a.reshape(n_seq, n_new, a.shape[-1]), ((0, 0), (0, tq - n_new), (0, 0)))
    o8 = _sample_attention(pad_t(q_pad), pad_t(gates), cmp_kv, cache_sel.reshape(n_phys, PAGE_SIZE, KV_WIDTH), page_table,
                           pad_t(kvs), state_win.reshape(n_seq, -1, KV_WIDTH), pad_t(kvw), tabs)
    o_pad = o8[:, :n_new].reshape(m, Q_PAD_WIDTH)
    u_tm = jnp.transpose(u.reshape(n_seq, n_new, SSM_WIDTH), (1, 0, 2))
    y_tm, s_re, s_im = _ssm(u_tm, s0_re.reshape(n_seq, SSM_LANES), s0_im.reshape(n_seq, SSM_LANES), lp["ssm"], n_new)
    y = jnp.transpose(y_tm, (1, 0, 2)).reshape(m, SSM_WIDTH)
    x2 = _merge(x, o_pad, y, g_mix, lp["w_mg"], lp["wa_pad"], w_branch_ssm, w_out, 1, m)
    return x2, kvc, kvs, kvw, s_re, s_im


def kernel(x_prompt, x_sample, cache_kv_cmp, cache_kv_sel, state_kv_win, state_ssm_re, state_ssm_im, page_table, p_prompt, p_sample, rel_bias, g_ffn1, w_ffn1_gate, w_ffn1_up, w_ffn1_down, g_mix, w_in, cmp_pos_k, cmp_w1_k, cmp_b1_k, cmp_w2_k, cmp_pos_v, cmp_w1_v, cmp_b1_v, cmp_w2_v, ssm_lambda_re, ssm_lambda_im, ssm_log_dt, ssm_b_re, ssm_b_im, ssm_c_re, ssm_c_im, ssm_d, w_glu, w_branch_attn, w_branch_ssm, w_out, g_ffn2, w_ffn2_gate, w_ffn2_up, w_ffn2_down, g_ple, w_ple_gate, w_ple_proj, g_final):
    depth = w_in.shape[0]
    n_batch, n_time, _ = x_prompt.shape
    n_seq, n_new, _ = x_sample.shape
    past_len = page_table.shape[1] * PAGE_SIZE
    win_len = state_kv_win.shape[2]
    tq = SUBLANES
    kv_row = (2, N_KV_HEADS, HEAD_DIM)

    xp = x_prompt.reshape(n_batch * n_time, D_MODEL)
    xs = x_sample.reshape(n_seq * n_new, D_MODEL)
    tabs_p = _prompt_tables(rel_bias, n_time)
    tabs_s = _sample_tables(rel_bias, past_len, n_new, tq, win_len)
    outs = [[] for _ in range(10)]
    for i in range(depth):
        ffn1 = (_row(g_ffn1[i]), w_ffn1_gate[i].astype(BF16), w_ffn1_up[i].astype(BF16), w_ffn1_down[i].astype(BF16))
        ffn2 = (_row(g_ffn2[i]), w_ffn2_gate[i].astype(BF16), w_ffn2_up[i].astype(BF16), w_ffn2_down[i].astype(BF16))
        cmp_params = (cmp_pos_k[i], cmp_w1_k[i], cmp_b1_k[i], cmp_w2_k[i], cmp_pos_v[i], cmp_w1_v[i], cmp_b1_v[i], cmp_w2_v[i])
        ssm_params = (ssm_lambda_re[i], ssm_lambda_im[i], ssm_log_dt[i], ssm_b_re[i], ssm_b_im[i],
                      ssm_c_re[i], ssm_c_im[i], ssm_d[i], w_glu[i])
        lp = _layer_params(w_in[i], cmp_params, ssm_params, w_branch_attn[i])
        wb, wo = w_branch_ssm[i].astype(BF16), w_out[i].astype(BF16)
        gm = _row(g_mix[i])

        xp = _half_ffn(xp, *ffn1)
        xs = _half_ffn(xs, *ffn1)
        xp, c_p, s_p, w_p, r_p, m_p = _mixer_prompt(xp, gm, lp, tabs_p, wb, wo, n_batch, n_time)
        xs, c_s, s_s, w_s, r_s, m_s = _mixer_sample(xs, gm, lp, tabs_s, wb, wo, cache_kv_cmp[i], cache_kv_sel[i],
                                                    state_kv_win[i], state_ssm_re[i], state_ssm_im[i], page_table,
                                                    n_seq, n_new, tq)
        xp = _half_ffn(xp, *ffn2)
        xs = _half_ffn(xs, *ffn2)
        ple = (_row(g_ple[i]), w_ple_gate[i].astype(BF16), w_ple_proj[i].astype(BF16))
        last = i == depth - 1
        gf = _row(g_final)
        xp = _ple_add(xp, p_prompt[i].reshape(-1, PLE_DIM), *ple, gf, last)
        xs = _ple_add(xs, p_sample[i].reshape(-1, PLE_DIM), *ple, gf, last)

        win_keep = min(WINDOW, n_time)
        outs[0].append(c_p.reshape((n_batch, n_time) + kv_row))
        outs[1].append(s_p.reshape((n_batch, n_time) + kv_row))
        outs[2].append(w_p.reshape((n_batch, n_time) + kv_row)[:, n_time - win_keep:])
        outs[3].append(r_p.reshape(n_batch, N_SSM_GROUPS, SSM_STATE))
        outs[4].append(m_p.reshape(n_batch, N_SSM_GROUPS, SSM_STATE))
        outs[5].append(c_s.reshape((n_seq, n_new) + kv_row))
        outs[6].append(s_s.reshape((n_seq, n_new) + kv_row))
        outs[7].append(jnp.concatenate([state_kv_win[i], w_s.reshape((n_seq, n_new) + kv_row)], axis=1)[:, n_new:])
        outs[8].append(r_s.reshape(n_seq, N_SSM_GROUPS, SSM_STATE))
        outs[9].append(m_s.reshape(n_seq, N_SSM_GROUPS, SSM_STATE))
    y_prompt = xp.reshape(n_batch, n_time, D_MODEL)
    y_sample = xs.reshape(n_seq, n_new, D_MODEL)
    return (y_prompt, y_sample) + tuple(jnp.stack(o) for o in outs)
```
